```python
import jax
import jax.numpy as jnp
from jax import lax
import numpy as np

D_MODEL = 1024
BATCH = 1
SEQ = 16384
DEPTH = 2
DEC_BATCH = 32
DEC_SEQ = 8
PAST_LEN = 16384
PAGE_SIZE = 128

HEAD_DIM = 64
D_MEM = D_MODEL // 4
N_HEADS_MEM = D_MEM // HEAD_DIM
N_MEM = 256
D_MOBA = D_MODEL - D_MEM
N_HEADS_MOBA = D_MOBA // HEAD_DIM
MOBA_BLOCK = 256
MOBA_TOPK = 3
QUERY_BLOCK = 128
ROPE_DIM = HEAD_DIM // 4
ROPE_THETA = 500000.0
D_POOL = D_MODEL - D_MEM
POOL_WINDOWS = (2, 4, 8, 16)
N_POOL_GROUPS = len(POOL_WINDOWS)
POOL_GROUP_DIM = D_POOL // N_POOL_GROUPS
POOL_STATE = max(POOL_WINDOWS) - 1
N_EXPERT_GROUPS = 4
EXPERTS_PER_GROUP = 8
N_EXPERTS = N_EXPERT_GROUPS * EXPERTS_PER_GROUP
EXPERT_TOPK = 2
D_EXPERT = D_MODEL // 4
RMS_EPS = 1e-6
NEG_INF = -1e30
N_MOBA_LAYERS = (DEPTH + 1) // 2
N_POOL_LAYERS = DEPTH // 2

kernel_name = 'moba_pool_hier_moe_hybrid_step'


def rmsnorm(x, g):
    xf = x.astype(jnp.float32)
    y = xf * lax.rsqrt(jnp.mean(xf * xf, axis=-1, keepdims=True) + RMS_EPS)
    return (y * g.astype(jnp.float32)).astype(x.dtype)


def rope_partial(x, pos):
    half = ROPE_DIM // 2
    inv_freq = ROPE_THETA ** (-jnp.arange(half, dtype=jnp.float32) / half)
    ang = pos.astype(jnp.float32)[:, None] * inv_freq[None, :]
    cos = jnp.cos(ang)[None, :, None, :]
    sin = jnp.sin(ang)[None, :, None, :]
    xr = x[..., :ROPE_DIM].astype(jnp.float32)
    x1, x2 = xr[..., :half], xr[..., half:]
    rot = jnp.concatenate([x1 * cos - x2 * sin, x2 * cos + x1 * sin], axis=-1).astype(x.dtype)
    return jnp.concatenate([rot, x[..., ROPE_DIM:]], axis=-1)


def moba_select(q, q_pos, kmean):
    B, Q, H, _ = q.shape
    nb = kmean.shape[2]
    own = q_pos // MOBA_BLOCK
    s = jnp.einsum('bqhd,bhnd->bhqn', q.astype(jnp.float32), kmean.astype(jnp.float32))
    fully_past = jnp.arange(nb)[None, :] < own[:, None]
    s = jnp.where(fully_past[None, None], s, NEG_INF)
    _, top = lax.top_k(s, min(MOBA_TOPK, nb))
    valid_top = top < own[None, None, :, None]
    own_b = jnp.broadcast_to(own[None, None, :, None], (B, H, Q, 1)).astype(top.dtype)
    idx = jnp.concatenate([top, own_b], axis=-1)
    valid = jnp.concatenate([valid_top, jnp.ones((B, H, Q, 1), dtype=bool)], axis=-1)
    return idx, valid


def moba_prompt(q, k, v):
    B, S, H, D = q.shape
    scale = HEAD_DIM ** -0.5
    nb = -(-S // MOBA_BLOCK)
    pad = nb * MOBA_BLOCK - S
    kb = jnp.pad(k, ((0, 0), (0, pad), (0, 0), (0, 0))).reshape(B, nb, MOBA_BLOCK, H, D).transpose(0, 3, 1, 2, 4)
    vb = jnp.pad(v, ((0, 0), (0, pad), (0, 0), (0, 0))).reshape(B, nb, MOBA_BLOCK, H, D).transpose(0, 3, 1, 2, 4)
    kmean = jnp.mean(kb.astype(jnp.float32), axis=3)
    bi = jnp.arange(B)[:, None, None, None]
    hi = jnp.arange(H)[None, :, None, None]
    nq = S // QUERY_BLOCK
    q_blocks = q.reshape(B, nq, QUERY_BLOCK, H, D).transpose(1, 0, 2, 3, 4)
    pos_blocks = jnp.arange(S).reshape(nq, QUERY_BLOCK)

    def one_block(args):
        qb, pb = args
        idx, valid = moba_select(qb, pb, kmean)
        ks = kb[bi, hi, idx]
        vs = vb[bi, hi, idx]
        key_pos = idx[..., None] * MOBA_BLOCK + jnp.arange(MOBA_BLOCK)
        mask = valid[..., None] & (key_pos <= pb[None, None, :, None, None])
        s = jnp.einsum('bqhd,bhqnkd->bhqnk', qb, ks, preferred_element_type=jnp.float32) * scale
        s = jnp.where(mask, s, NEG_INF)
        p = jax.nn.softmax(s.reshape(B, H, QUERY_BLOCK, -1), axis=-1).reshape(s.shape).astype(vs.dtype)
        return jnp.einsum('bhqnk,bhqnkd->bqhd', p, vs)

    o = lax.map(one_block, (q_blocks, pos_blocks))
    return o.transpose(1, 0, 2, 3, 4).reshape(B, S, H * D)


def moba_sample(q, k_new, v_new, pool_k, pool_v, page_table):
    B, S, H, D = q.shape
    scale = HEAD_DIM ** -0.5
    n_pages = page_table.shape[1]
    past = n_pages * PAGE_SIZE
    ppb = MOBA_BLOCK // PAGE_SIZE
    q_pos = past + jnp.arange(S)
    nb = -(-(past + S) // MOBA_BLOCK)
    page_sum = jnp.sum(pool_k.astype(jnp.float32), axis=1)[page_table]
    page_sum = jnp.pad(page_sum, ((0, 0), (0, nb * ppb - n_pages), (0, 0), (0, 0)))
    blk_sum = page_sum.reshape(B, nb, ppb, H, D).sum(axis=2)
    new_blk = q_pos // MOBA_BLOCK
    onehot = (new_blk[:, None] == jnp.arange(nb)[None, :]).astype(jnp.float32)
    blk_sum = blk_sum + jnp.einsum('sn,bshd->bnhd', onehot, k_new.astype(jnp.float32))
    kmean = jnp.transpose(blk_sum, (0, 2, 1, 3)) / MOBA_BLOCK
    idx, valid = moba_select(q, q_pos, kmean)
    lp = idx[..., None] * ppb + jnp.arange(ppb)
    in_past = lp < n_pages
    bi = jnp.arange(B)[:, None, None, None, None]
    hi = jnp.arange(H)[None, :, None, None, None]
    phys = page_table[bi, jnp.minimum(lp, n_pages - 1)]
    ks = pool_k[phys, :, hi]
    vs = pool_v[phys, :, hi]
    key_pos = lp[..., None] * PAGE_SIZE + jnp.arange(PAGE_SIZE)
    mask_p = (valid[..., None] & in_past)[..., None] & (key_pos <= q_pos[None, None, :, None, None, None])
    s_p = jnp.einsum('bqhd,bhqnpkd->bhqnpk', q, ks, preferred_element_type=jnp.float32) * scale
    s_p = jnp.where(mask_p, s_p, NEG_INF).reshape(B, H, S, -1)
    s_n = jnp.einsum('bqhd,bkhd->bhqk', q, k_new, preferred_element_type=jnp.float32) * scale
    member = jnp.any((idx[..., None] == new_blk[None, None, None, None, :]) & valid[..., None], axis=3)
    causal = jnp.arange(S)[:, None] >= jnp.arange(S)[None, :]
    s_n = jnp.where(member & causal[None, None], s_n, NEG_INF)
    n_past = s_p.shape[-1]
    p = jax.nn.softmax(jnp.concatenate([s_p, s_n], axis=-1), axis=-1)
    p_p = p[..., :n_past].reshape(mask_p.shape).astype(vs.dtype)
    p_n = p[..., n_past:].astype(v_new.dtype)
    o = jnp.einsum('bhqnpk,bhqnpkd->bqhd', p_p, vs) + jnp.einsum('bhqk,bkhd->bqhd', p_n, v_new)
    return o.reshape(B, S, H * D)


def pool_mix(u, prefix, pos, pool_w, pool_scale):
    B, S, _ = u.shape
    ext = jnp.concatenate([prefix.astype(u.dtype), u], axis=1)
    csum = jnp.pad(jnp.cumsum(ext.astype(jnp.float32), axis=1), ((0, 0), (1, 0), (0, 0)))
    uf = u.astype(jnp.float32)
    diffs = []
    for g, w in enumerate(POOL_WINDOWS):
        c0 = g * POOL_GROUP_DIM
        c1 = c0 + POOL_GROUP_DIM
        win = csum[:, POOL_STATE + 1:, c0:c1] - csum[:, POOL_STATE + 1 - w:POOL_STATE + 1 - w + S, c0:c1]
        count = jnp.minimum(pos + 1, w).astype(jnp.float32)[None, :, None]
        diffs.append(win / count - uf[:, :, c0:c1])
    d = jnp.stack(diffs, axis=2)
    y = jnp.einsum('bsgc,gce->bsge', d, pool_w.astype(jnp.float32)).reshape(B, S, D_POOL)
    y = (y * pool_scale.astype(jnp.float32)).astype(u.dtype)
    return y, ext[:, -POOL_STATE:]


def mem_kv(mem, g, w):
    B, N, _ = mem.shape
    kv = rmsnorm(mem, g) @ w
    k, v = jnp.split(kv, 2, axis=-1)
    return k.reshape(B, N, N_HEADS_MEM, HEAD_DIM), v.reshape(B, N, N_HEADS_MEM, HEAD_DIM)


def mem_attend(qm, mk, mv):
    B, S, _, _ = qm.shape
    s = jnp.einsum('bqhd,bkhd->bhqk', qm, mk, preferred_element_type=jnp.float32) * (HEAD_DIM ** -0.5)
    p = jax.nn.softmax(s, axis=-1).astype(mv.dtype)
    return jnp.einsum('bhqk,bkhd->bqhd', p, mv).reshape(B, S, D_MEM)


def hier_moe(x, wg, bg, we, be, w_gate, w_up, w_down):
    B, S, _ = x.shape
    lg = jnp.einsum('bsd,dg->bsg', x, wg, preferred_element_type=jnp.float32) + bg.astype(jnp.float32)
    pg = jax.nn.softmax(lg, axis=-1)
    _, gi = lax.top_k(lg, 1)
    gmask = jax.nn.one_hot(gi[..., 0], N_EXPERT_GROUPS, dtype=jnp.float32)
    pg_sel = jnp.sum(pg * gmask, axis=-1, keepdims=True)
    le = jnp.einsum('bsd,de->bse', x, we, preferred_element_type=jnp.float32) + be.astype(jnp.float32)
    le_g = jnp.einsum('bsge,bsg->bse', le.reshape(B, S, N_EXPERT_GROUPS, EXPERTS_PER_GROUP), gmask)
    tv, ti = lax.top_k(le_g, EXPERT_TOPK)
    w2 = jax.nn.softmax(tv, axis=-1) * pg_sel
    expert_id = gi * EXPERTS_PER_GROUP + ti
    gate = jnp.sum(jax.nn.one_hot(expert_id, N_EXPERTS, dtype=jnp.float32) * w2[..., None], axis=-2)
    h = jax.nn.silu(jnp.einsum('bsd,edf->bsef', x, w_gate)) * jnp.einsum('bsd,edf->bsef', x, w_up)
    h = h * gate[..., None].astype(h.dtype)
    return jnp.einsum('bsef,efd->bsd', h, w_down)


def run_trunk(x, pos, attend_moba, pool_prefix, mem_k, mem_v, norm1_g, norm2_g, final_norm_g,
              w_in_moba, w_out_moba, w_in_pool, pool_w, pool_scale, w_out_pool,
              router_group_w, router_group_b, router_expert_w, router_expert_b, w_gate, w_up, w_down):
    B, S, _ = x.shape
    k_rows, v_rows, pool_states = [], [], []
    for i in range(DEPTH):
        j = i // 2
        h = rmsnorm(x, norm1_g[i])
        if i % 2 == 0:
            proj = h @ w_in_moba[j]
            q, k, v, qm = jnp.split(proj, [D_MOBA, 2 * D_MOBA, 3 * D_MOBA], axis=-1)
            q = rope_partial(q.reshape(B, S, N_HEADS_MOBA, HEAD_DIM), pos)
            k = rope_partial(k.reshape(B, S, N_HEADS_MOBA, HEAD_DIM), pos)
            v = v.reshape(B, S, N_HEADS_MOBA, HEAD_DIM)
            o_tok = attend_moba(j, q, k, v)
            k_rows.append(k)
            v_rows.append(v)
            w_out = w_out_moba[j]
        else:
            proj = h @ w_in_pool[j]
            u, qm = jnp.split(proj, [D_POOL], axis=-1)
            o_tok, st = pool_mix(u, pool_prefix[j], pos, pool_w[j], pool_scale[j])
            pool_states.append(st)
            w_out = w_out_pool[j]
        o_mem = mem_attend(qm.reshape(B, S, N_HEADS_MEM, HEAD_DIM), mem_k[i], mem_v[i])
        x = x + jnp.concatenate([o_tok, o_mem], axis=-1) @ w_out
        x = x + hier_moe(rmsnorm(x, norm2_g[i]), router_group_w[i], router_group_b[i],
                         router_expert_w[i], router_expert_b[i], w_gate[i], w_up[i], w_down[i])
    return rmsnorm(x, final_norm_g), jnp.stack(k_rows), jnp.stack(v_rows), jnp.stack(pool_states)


def setup_inputs(seed: int = 0) -> dict:
    key = jax.random.key(seed)
    ks = jax.random.split(key, 32)
    f32 = jnp.float32
    n_pages = PAST_LEN // PAGE_SIZE
    n_used = DEC_BATCH * n_pages
    n_phys = n_used + n_used // 4

    def nrm(k, shape, scale=1.0):
        return jax.random.normal(k, shape, f32) * scale

    page_table = jax.random.permutation(ks[0], n_phys)[:n_used].reshape(DEC_BATCH, n_pages).astype(jnp.int32)
    sd = D_MODEL ** -0.5
    return {
        'x_prompt': nrm(ks[1], (BATCH, SEQ, D_MODEL)),
        'x_sample': nrm(ks[2], (DEC_BATCH, DEC_SEQ, D_MODEL)),
        'cache_moba_k': nrm(ks[3], (N_MOBA_LAYERS, n_phys, PAGE_SIZE, N_HEADS_MOBA, HEAD_DIM)),
        'cache_moba_v': nrm(ks[4], (N_MOBA_LAYERS, n_phys, PAGE_SIZE, N_HEADS_MOBA, HEAD_DIM)),
        'state_pool': nrm(ks[5], (N_POOL_LAYERS, DEC_BATCH, POOL_STATE, D_POOL)),
        'cache_mem_k': nrm(ks[6], (DEPTH, DEC_BATCH, N_MEM, N_HEADS_MEM, HEAD_DIM)),
        'cache_mem_v': nrm(ks[7], (DEPTH, DEC_BATCH, N_MEM, N_HEADS_MEM, HEAD_DIM)),
        'page_table': page_table,
        'mem_prompt': nrm(ks[8], (BATCH, N_MEM, D_MODEL)),
        'norm1_g': 1.0 + nrm(ks[9], (DEPTH, D_MODEL), 0.02),
        'norm2_g': 1.0 + nrm(ks[10], (DEPTH, D_MODEL), 0.02),
        'norm_mem_g': 1.0 + nrm(ks[11], (DEPTH, D_MODEL), 0.02),
        'final_norm_g': 1.0 + nrm(ks[12], (D_MODEL,), 0.02),
        'w_in_moba': nrm(ks[13], (N_MOBA_LAYERS, D_MODEL, 3 * D_MOBA + D_MEM), sd),
        'w_out_moba': nrm(ks[14], (N_MOBA_LAYERS, D_MODEL, D_MODEL), sd),
        'w_in_pool': nrm(ks[15], (N_POOL_LAYERS, D_MODEL, D_POOL + D_MEM), sd),
        'pool_w': nrm(ks[16], (N_POOL_LAYERS, N_POOL_GROUPS, POOL_GROUP_DIM, POOL_GROUP_DIM), POOL_GROUP_DIM ** -0.5),
        'pool_scale': 1.0 + nrm(ks[17], (N_POOL_LAYERS, D_POOL), 0.02),
        'w_out_pool': nrm(ks[18], (N_POOL_LAYERS, D_MODEL, D_MODEL), sd),
        'w_mem_kv': nrm(ks[19], (DEPTH, D_MODEL, 2 * D_MEM), sd),
        'router_group_w': nrm(ks[20], (DEPTH, D_MODEL, N_EXPERT_GROUPS), sd),
        'router_group_b': nrm(ks[21], (DEPTH, N_EXPERT_GROUPS), 0.01),
        'router_expert_w': nrm(ks[22], (DEPTH, D_MODEL, N_EXPERTS), sd),
        'router_expert_b': nrm(ks[23], (DEPTH, N_EXPERTS), 0.01),
        'w_gate': nrm(ks[24], (DEPTH, N_EXPERTS, D_MODEL, D_EXPERT), sd),
        'w_up': nrm(ks[25], (DEPTH, N_EXPERTS, D_MODEL, D_EXPERT), sd),
        'w_down': nrm(ks[26], (DEPTH, N_EXPERTS, D_EXPERT, D_MODEL), D_EXPERT ** -0.5),
    }


def reference(x_prompt, x_sample, cache_moba_k, cache_moba_v, state_pool, cache_mem_k, cache_mem_v,
              page_table, mem_prompt, norm1_g, norm2_g, norm_mem_g, final_norm_g,
              w_in_moba, w_out_moba, w_in_pool, pool_w, pool_scale, w_out_pool, w_mem_kv,
              router_group_w, router_group_b, router_expert_w, router_expert_b, w_gate, w_up, w_down):
    shared = (norm1_g, norm2_g, final_norm_g, w_in_moba, w_out_moba, w_in_pool, pool_w, pool_scale,
              w_out_pool, router_group_w, router_group_b, router_expert_w, router_expert_b, w_gate, w_up, w_down)
    B, S, _ = x_prompt.shape
    mem_k_list, mem_v_list = [], []
    for i in range(DEPTH):
        mk, mv = mem_kv(mem_prompt, norm_mem_g[i], w_mem_kv[i])
        mem_k_list.append(mk)
        mem_v_list.append(mv)
    mem_k_prompt = jnp.stack(mem_k_list)
    mem_v_prompt = jnp.stack(mem_v_list)
    pos_prompt = jnp.arange(S)
    zero_prefix = jnp.zeros((N_POOL_LAYERS, B, POOL_STATE, D_POOL), dtype=x_prompt.dtype)
    y_prompt, moba_k_prompt, moba_v_prompt, pool_state_prompt = run_trunk(
        x_prompt, pos_prompt, lambda j, q, k, v: moba_prompt(q, k, v), zero_prefix,
        mem_k_prompt, mem_v_prompt, *shared)
    past = page_table.shape[1] * PAGE_SIZE
    pos_sample = past + jnp.arange(x_sample.shape[1])
    y_sample, moba_k_sample, moba_v_sample, pool_state_sample = run_trunk(
        x_sample, pos_sample,
        lambda j, q, k, v: moba_sample(q, k, v, cache_moba_k[j], cache_moba_v[j], page_table),
        state_pool, cache_mem_k, cache_mem_v, *shared)
    return (y_prompt, y_sample, moba_k_prompt, moba_v_prompt, pool_state_prompt, mem_k_prompt, mem_v_prompt,
            moba_k_sample, moba_v_sample, pool_state_sample)
```

```python
import functools

import jax
import jax.numpy as jnp
from jax import lax
from jax.experimental import pallas as pl
from jax.experimental.pallas import tpu as pltpu

F32 = jnp.float32
BF16 = jnp.bfloat16
I32 = jnp.int32

D_MODEL = 1024
HEAD_DIM = 64
D_MEM = 256
N_MEM = 256
D_MOBA = 768
N_HEADS_MOBA = 12
N_HEAD_PAIRS = 6
MOBA_BLOCK = 256
MOBA_TOPK = 3
PAGE_SIZE = 128
PAGES_PER_BLOCK = MOBA_BLOCK // PAGE_SIZE
ROPE_DIM = 16
ROPE_THETA = 500000.0
D_POOL = 768
POOL_WINDOWS = (2, 4, 8, 16)
POOL_GROUP_DIM = 192
POOL_STATE = 15
POOL_HALO = 16
N_EXPERTS = 32
EXPERTS_PER_GROUP = 8
N_EXPERT_GROUPS = 4
D_EXPERT = 256
RMS_EPS = 1e-6
NEG_INF = -1e30
SCALE = HEAD_DIM ** -0.5

LANES = 128
TM = 256
VMEM_LIMIT = 56 * 1024 * 1024

NT_DIMS = (((1,), (1,)), ((), ()))


def _params(*sem):
    return pltpu.CompilerParams(dimension_semantics=sem, vmem_limit_bytes=VMEM_LIMIT)


def _rms(x, g):
    return x * lax.rsqrt(jnp.mean(x * x, axis=-1, keepdims=True) + RMS_EPS) * g


def _first_max(vals, lane_f):
    m = jnp.max(vals, axis=1, keepdims=True)
    idx = jnp.min(jnp.where(vals == m, lane_f, 1e9), axis=1, keepdims=True)
    return m, idx


def _topk_lanes(g, valid, lane_f, k):
    cur = jnp.where(valid, g, NEG_INF)
    picks = []
    for _ in range(k):
        _, idx = _first_max(cur, lane_f)
        picks.append(idx)
        cur = jnp.where(lane_f == idx, -jnp.inf, cur)
    return picks


def _rope_chunks(a, c, s1, s2):
    out = []
    for j in range(a.shape[1] // LANES):
        ac = a[:, j * LANES:(j + 1) * LANES]
        out.append(ac * c + pltpu.roll(ac, LANES - ROPE_DIM // 2, 1) * s1 + pltpu.roll(ac, ROPE_DIM // 2, 1) * s2)
    return jnp.concatenate(out, axis=1)


def _in_proj_moba_body(x_ref, g_ref, w_ref, c_ref, s1_ref, s2_ref,
                       q_ref, k_ref, v_ref, qm_ref, kb_ref, vb_ref, ks_ref):
    xb = _rms(x_ref[...], g_ref[...]).astype(BF16)
    c, s1, s2 = c_ref[...], s1_ref[...], s2_ref[...]
    q = _rope_chunks(jnp.dot(xb, w_ref[:, 0:D_MOBA], preferred_element_type=F32), c, s1, s2)
    k = _rope_chunks(jnp.dot(xb, w_ref[:, D_MOBA:2 * D_MOBA], preferred_element_type=F32), c, s1, s2)
    v = jnp.dot(xb, w_ref[:, 2 * D_MOBA:3 * D_MOBA], preferred_element_type=F32)
    q_ref[...] = q
    k_ref[...] = k
    v_ref[...] = v
    qm_ref[...] = jnp.dot(xb, w_ref[:, 3 * D_MOBA:], preferred_element_type=F32)
    kb_ref[...] = k.astype(BF16)
    vb_ref[...] = v.astype(BF16)
    ks_ref[...] = (jnp.sum(k, axis=0, keepdims=True) * (1.0 / MOBA_BLOCK))[None]


def _in_proj_moba(x, g, w, c, s1, s2):
    t = x.shape[0]
    nt = t // TM
    n = w.shape[1]
    row = lambda i: (i, 0)
    fix = lambda i: (0, 0)
    return pl.pallas_call(
        _in_proj_moba_body,
        grid=(nt,),
        in_specs=[pl.BlockSpec((TM, D_MODEL), row), pl.BlockSpec((1, D_MODEL), fix),
                  pl.BlockSpec((D_MODEL, n), fix),
                  pl.BlockSpec((TM, LANES), row), pl.BlockSpec((TM, LANES), row), pl.BlockSpec((TM, LANES), row)],
        out_specs=[pl.BlockSpec((TM, D_MOBA), row), pl.BlockSpec((TM, D_MOBA), row), pl.BlockSpec((TM, D_MOBA), row),
                   pl.BlockSpec((TM, D_MEM), row), pl.BlockSpec((TM, D_MOBA), row), pl.BlockSpec((TM, D_MOBA), row),
                   pl.BlockSpec((1, 1, D_MOBA), lambda i: (i, 0, 0))],
        out_shape=[jax.ShapeDtypeStruct((t, D_MOBA), F32), jax.ShapeDtypeStruct((t, D_MOBA), F32),
                   jax.ShapeDtypeStruct((t, D_MOBA), F32), jax.ShapeDtypeStruct((t, D_MEM), F32),
                   jax.ShapeDtypeStruct((t, D_MOBA), BF16), jax.ShapeDtypeStruct((t, D_MOBA), BF16),
                   jax.ShapeDtypeStruct((nt, 1, D_MOBA), F32)],
        compiler_params=_params("parallel"),
        name="in_proj_moba",
    )(x, g, w, c, s1, s2)


def _in_proj_split_body(x_ref, g_ref, w_ref, a_ref, b_ref, *, split):
    xb = _rms(x_ref[...], g_ref[...]).astype(BF16)
    a_ref[...] = jnp.dot(xb, w_ref[:, 0:split], preferred_element_type=F32)
    b_ref[...] = jnp.dot(xb, w_ref[:, split:], preferred_element_type=F32)


def _in_proj_split(x, g, w, split):
    t = x.shape[0]
    n = w.shape[1]
    row = lambda i: (i, 0)
    fix = lambda i: (0, 0)
    return pl.pallas_call(
        functools.partial(_in_proj_split_body, split=split),
        grid=(t // TM,),
        in_specs=[pl.BlockSpec((TM, D_MODEL), row), pl.BlockSpec((1, D_MODEL), fix), pl.BlockSpec((D_MODEL, n), fix)],
        out_specs=[pl.BlockSpec((TM, split), row), pl.BlockSpec((TM, n - split), row)],
        out_shape=[jax.ShapeDtypeStruct((t, split), F32), jax.ShapeDtypeStruct((t, n - split), F32)],
        compiler_params=_params("parallel"),
        name="in_proj_split",
    )(x, g, w)


def _moba_prompt_body(q_ref, k_ref, v_ref, km_ref, o_ref, m_ref, l_ref, acc_ref):
    qi = pl.program_id(1)
    q = q_ref[...]
    lane = lax.broadcasted_iota(I32, (MOBA_BLOCK, LANES), 1)
    lane_f = lane.astype(F32)
    row_k = lax.broadcasted_iota(I32, (MOBA_BLOCK, MOBA_BLOCK), 0)
    col_k = lax.broadcasted_iota(I32, (MOBA_BLOCK, MOBA_BLOCK), 1)
    heads = []
    for hh in range(2):
        hmask = (lane < HEAD_DIM) if hh == 0 else (lane >= HEAD_DIM)
        qh = jnp.where(hmask, q, 0.0)
        gate = lax.dot_general(qh, km_ref[...], NT_DIMS, precision=lax.Precision.HIGHEST,
                               preferred_element_type=F32)
        valid = lane < qi
        sel = jnp.zeros((MOBA_BLOCK, LANES), F32)
        for idx in _topk_lanes(gate, valid, lane_f, MOBA_TOPK):
            sel = jnp.where(lane_f == idx, 1.0, sel)
        sel = jnp.where(valid, sel, 0.0)
        qb = (qh * SCALE).astype(BF16)
        m_ref[...] = jnp.full((MOBA_BLOCK, 1), NEG_INF, F32)
        l_ref[...] = jnp.zeros((MOBA_BLOCK, 1), F32)
        acc_ref[...] = jnp.zeros((MOBA_BLOCK, LANES), F32)

        def update(s, keep, vj):
            s = jnp.where(keep, s, NEG_INF)
            m_old = m_ref[...]
            m_new = jnp.maximum(m_old, jnp.max(s, axis=1, keepdims=True))
            alpha = jnp.exp(m_old - m_new)
            p = jnp.where(keep, jnp.exp(s - m_new), 0.0)
            l_ref[...] = alpha * l_ref[...] + jnp.sum(p, axis=1, keepdims=True)
            acc_ref[...] = alpha * acc_ref[...] + jnp.dot(p.astype(BF16), vj, preferred_element_type=F32)
            m_ref[...] = m_new

        def past_block(j, carry):
            off = pl.multiple_of(j * MOBA_BLOCK, MOBA_BLOCK)
            kj = k_ref[pl.ds(off, MOBA_BLOCK), :]
            vj = v_ref[pl.ds(off, MOBA_BLOCK), :]
            s = lax.dot_general(qb, kj, NT_DIMS, preferred_element_type=F32)
            chosen = jnp.sum(jnp.where(lane == j, sel, 0.0), axis=1, keepdims=True) > 0.5
            update(s, chosen, vj)
            return carry

        lax.fori_loop(0, qi, past_block, 0)
        off = pl.multiple_of(qi * MOBA_BLOCK, MOBA_BLOCK)
        kj = k_ref[pl.ds(off, MOBA_BLOCK), :]
        vj = v_ref[pl.ds(off, MOBA_BLOCK), :]
        s = lax.dot_general(qb, kj, NT_DIMS, preferred_element_type=F32)
        update(s, col_k <= row_k, vj)
        heads.append(acc_ref[...] / l_ref[...])
    o_ref[...] = jnp.where(lane < HEAD_DIM, heads[0], heads[1])


def _moba_prompt(q, kb, vb, kmean, s_len):
    nq = s_len // MOBA_BLOCK
    return pl.pallas_call(
        _moba_prompt_body,
        grid=(N_HEAD_PAIRS, nq),
        in_specs=[pl.BlockSpec((MOBA_BLOCK, LANES), lambda h, i: (i, h)),
                  pl.BlockSpec((s_len, LANES), lambda h, i: (0, h)),
                  pl.BlockSpec((s_len, LANES), lambda h, i: (0, h)),
                  pl.BlockSpec((LANES, LANES), lambda h, i: (0, h))],
        out_specs=pl.BlockSpec((MOBA_BLOCK, LANES), lambda h, i: (i, h)),
        out_shape=jax.ShapeDtypeStruct((s_len, D_MOBA), F32),
        scratch_shapes=[pltpu.VMEM((MOBA_BLOCK, 1), F32), pltpu.VMEM((MOBA_BLOCK, 1), F32),
                        pltpu.VMEM((MOBA_BLOCK, LANES), F32)],
        compiler_params=_params("parallel", "arbitrary"),
        name="moba_prompt",
    )(q, kb, vb, kmean)


SEL_PAGES = 16
SEL_BLOCKS = SEL_PAGES // PAGES_PER_BLOCK


def _moba_select_body(pt_ref, *refs, n_chunks):
    page_refs = refs[:SEL_PAGES]
    q_ref, idx_ref, acc_ref = refs[SEL_PAGES:]
    c = pl.program_id(1)
    lane = lax.broadcasted_iota(I32, (HEAD_DIM, LANES), 1)

    @pl.when(c == 0)
    def _():
        acc_ref[...] = jnp.zeros_like(acc_ref)

    for bl in range(SEL_BLOCKS):
        n = c * SEL_BLOCKS + bl
        for h in range(N_HEADS_MOBA):
            t = page_refs[2 * bl][h] + page_refs[2 * bl + 1][h]
            col = jnp.sum(t, axis=1, keepdims=True)
            acc_ref[h] = acc_ref[h] + jnp.where(lane == n, col, 0.0)

    @pl.when(c == n_chunks - 1)
    def _():
        q = q_ref[...]
        rows = q.shape[0]
        ql = lax.broadcasted_iota(I32, (rows, LANES), 1)
        ql_f = ql.astype(F32)
        for hp in range(N_HEAD_PAIRS):
            qp = q[:, hp * LANES:(hp + 1) * LANES]
            ksum = jnp.concatenate([acc_ref[2 * hp], acc_ref[2 * hp + 1]], axis=0)
            for hh in range(2):
                hmask = (ql < HEAD_DIM) if hh == 0 else (ql >= HEAD_DIM)
                gate = jnp.dot(jnp.where(hmask, qp, 0.0), ksum, precision=lax.Precision.HIGHEST,
                               preferred_element_type=F32) * (1.0 / MOBA_BLOCK)
                picks = _topk_lanes(gate, ql < n_chunks * SEL_BLOCKS, ql_f, MOBA_TOPK)
                out = jnp.zeros((rows, LANES), F32)
                for r, idx in enumerate(picks):
                    out = jnp.where(ql == r, idx, out)
                idx_ref[2 * hp + hh] = out.astype(I32)


def _moba_select(page_table, kc, q, dec, row_block0):
    nb, n_pages = page_table.shape
    n_chunks = n_pages // SEL_PAGES
    page_spec = lambda i: pl.BlockSpec((None, N_HEADS_MOBA, HEAD_DIM, PAGE_SIZE),
                                       lambda b, c, pt: (pt[b, c * SEL_PAGES + i], 0, 0, 0))
    grid_spec = pltpu.PrefetchScalarGridSpec(
        num_scalar_prefetch=1,
        grid=(nb, n_chunks),
        in_specs=[page_spec(i) for i in range(SEL_PAGES)]
        + [pl.BlockSpec((dec, D_MOBA), lambda b, c, pt: (row_block0 + b, 0))],
        out_specs=pl.BlockSpec((None, N_HEADS_MOBA, dec, LANES), lambda b, c, pt: (b, 0, 0, 0)),
        scratch_shapes=[pltpu.VMEM((N_HEADS_MOBA, HEAD_DIM, LANES), F32)],
    )
    return pl.pallas_call(
        functools.partial(_moba_select_body, n_chunks=n_chunks),
        grid_spec=grid_spec,
        out_shape=jax.ShapeDtypeStruct((nb, N_HEADS_MOBA, dec, LANES), I32),
        compiler_params=_params("parallel", "arbitrary"),
        name="moba_select",
    )(page_table, *([kc] * SEL_PAGES), q)


N_SEL_PAGES = MOBA_TOPK * PAGES_PER_BLOCK


def _moba_sample_body(ph_ref, *refs):
    k_refs = refs[:N_SEL_PAGES]
    v_refs = refs[N_SEL_PAGES:2 * N_SEL_PAGES]
    qt_ref, knt_ref, vnt_ref, o_ref = refs[2 * N_SEL_PAGES:]
    qq = pl.program_id(2)
    dec = qt_ref.shape[1]
    lane_d = lax.broadcasted_iota(I32, (HEAD_DIM, dec), 1)
    qcol = jnp.sum(jnp.where(lane_d == qq, qt_ref[...], 0.0), axis=1, keepdims=True) * SCALE
    s_past = [jnp.sum(k_refs[i][...] * qcol, axis=0, keepdims=True) for i in range(N_SEL_PAGES)]
    lane_n = lax.broadcasted_iota(I32, (1, dec), 1)
    causal = lane_n <= qq
    s_new = jnp.where(causal, jnp.sum(knt_ref[...] * qcol, axis=0, keepdims=True), NEG_INF)
    m = jnp.max(s_new, axis=1, keepdims=True)
    for s in s_past:
        m = jnp.maximum(m, jnp.max(s, axis=1, keepdims=True))
    p_new = jnp.where(causal, jnp.exp(s_new - m), 0.0)
    l = jnp.sum(p_new, axis=1, keepdims=True)
    o = jnp.sum(vnt_ref[...] * p_new, axis=1, keepdims=True)
    for i in range(N_SEL_PAGES):
        p = jnp.exp(s_past[i] - m)
        l = l + jnp.sum(p, axis=1, keepdims=True)
        o = o + jnp.sum(v_refs[i][...] * p, axis=1, keepdims=True)
    o = o / l

    @pl.when(qq == 0)
    def _():
        o_ref[...] = jnp.zeros_like(o_ref)

    o_ref[...] = jnp.where(lane_d == qq, o, o_ref[...])


def _moba_sample(phys, kc, vc, qt, knt, vnt):
    nb, nh, _, dec = qt.shape

    def page_spec(i):
        return pl.BlockSpec((None, None, HEAD_DIM, PAGE_SIZE),
                            lambda b, h, qq, ph: (ph[((b * nh + h) * dec + qq) * N_SEL_PAGES + i], h, 0, 0))

    small = pl.BlockSpec((None, None, HEAD_DIM, dec), lambda b, h, qq, ph: (b, h, 0, 0))
    grid_spec = pltpu.PrefetchScalarGridSpec(
        num_scalar_prefetch=1,
        grid=(nb, nh, dec),
        in_specs=[page_spec(i) for i in range(N_SEL_PAGES)] * 2 + [small, small, small],
        out_specs=small,
    )
    return pl.pallas_call(
        _moba_sample_body,
        grid_spec=grid_spec,
        out_shape=jax.ShapeDtypeStruct((nb, nh, HEAD_DIM, dec), F32),
        compiler_params=_params("parallel", "parallel", "arbitrary"),
        name="moba_sample",
    )(phys, *([kc] * N_SEL_PAGES), *([vc] * N_SEL_PAGES), qt, knt, vnt)


def _mem_attn_body(q_ref, kt_ref, vt_ref, o_ref):
    q = q_ref[...]
    tm = q.shape[0]
    lane = lax.broadcasted_iota(I32, (tm, LANES), 1)
    pairs = []
    for p in range(2):
        qp = q[:, p * LANES:(p + 1) * LANES]
        kt = kt_ref[p].astype(BF16)
        vt = vt_ref[p].astype(BF16)
        heads = []
        for hh in range(2):
            hmask = (lane < HEAD_DIM) if hh == 0 else (lane >= HEAD_DIM)
            qh = (jnp.where(hmask, qp, 0.0) * SCALE).astype(BF16)
            s = jnp.dot(qh, kt, preferred_element_type=F32)
            e = jnp.exp(s - jnp.max(s, axis=1, keepdims=True))
            prob = (e / jnp.sum(e, axis=1, keepdims=True)).astype(BF16)
            heads.append(lax.dot_general(prob, vt, NT_DIMS, preferred_element_type=F32))
        pairs.append(jnp.where(lane < HEAD_DIM, heads[0], heads[1]))
    o_ref[...] = jnp.concatenate(pairs, axis=1)


def _mem_attn(qm, kt, vt, tm, n_tiles, row_block0, per_tile_mem):
    mem_idx = (lambda i: (i, 0, 0, 0)) if per_tile_mem else (lambda i: (0, 0, 0, 0))
    mem_spec = pl.BlockSpec((None, 2, LANES, N_MEM), mem_idx)
    return pl.pallas_call(
        _mem_attn_body,
        grid=(n_tiles,),
        in_specs=[pl.BlockSpec((tm, D_MEM), lambda i: (row_block0 + i, 0)), mem_spec, mem_spec],
        out_specs=pl.BlockSpec((tm, D_MEM), lambda i: (i, 0)),
        out_shape=jax.ShapeDtypeStruct((tm * n_tiles, D_MEM), F32),
        compiler_params=_params("parallel"),
        name="mem_attn",
    )(qm, kt, vt)


def _pool_body(u_ref, halo_ref, w_ref, scale_ref, o_ref, *, sequence_start, full_windows):
    i = pl.program_id(0)
    u = u_ref[...]
    tm = u.shape[0]
    halo = halo_ref[...]
    if sequence_start:
        halo = jnp.where(i == 0, 0.0, halo)
    ext = jnp.concatenate([halo, u], axis=0)
    sums = []
    run = ext
    for shift in (1, 2, 4, 8):
        run = run + pltpu.roll(run, shift, 0)
        sums.append(run[POOL_HALO:])
    lane = lax.broadcasted_iota(I32, (tm, D_POOL), 1)
    pos = (lax.broadcasted_iota(I32, (tm, 1), 0) + i * tm + 1).astype(F32)
    d = None
    for g in reversed(range(len(POOL_WINDOWS))):
        w = float(POOL_WINDOWS[g])
        count = w if full_windows else jnp.minimum(pos, w)
        mean = sums[g] / count
        d = mean if d is None else jnp.where(lane < (g + 1) * POOL_GROUP_DIM, mean, d)
    d = d - u
    o_ref[...] = jnp.dot(d.astype(BF16), w_ref[...], preferred_element_type=F32) * scale_ref[...]


def _pool_mix(u, halo_src, w_bd, scale, tm, n_tiles, row_block0, halo_index, sequence_start, full_windows):
    fix = lambda i: (0, 0)
    if halo_src.ndim == 3:
        halo_spec = pl.BlockSpec((None, POOL_HALO, D_POOL), halo_index)
    else:
        halo_spec = pl.BlockSpec((POOL_HALO, D_POOL), halo_index)
    return pl.pallas_call(
        functools.partial(_pool_body, sequence_start=sequence_start, full_windows=full_windows),
        grid=(n_tiles,),
        in_specs=[pl.BlockSpec((tm, D_POOL), lambda i: (row_block0 + i, 0)), halo_spec,
                  pl.BlockSpec((D_POOL, D_POOL), fix), pl.BlockSpec((1, D_POOL), fix)],
        out_specs=pl.BlockSpec((tm, D_POOL), lambda i: (i, 0)),
        out_shape=jax.ShapeDtypeStruct((tm * n_tiles, D_POOL), F32),
        compiler_params=_params("parallel"),
        name="pool_mix",
    )(u, halo_src, w_bd, scale)


def _out_proj_router_body(x_ref, ot_ref, om_ref, wo_ref, g_ref, wr_ref, br_ref, x1_ref, xn_ref, route_ref):
    d_tok = ot_ref.shape[1]
    a = jnp.dot(ot_ref[...].astype(BF16), wo_ref[0:d_tok, :], preferred_element_type=F32)
    a = a + jnp.dot(om_ref[...].astype(BF16), wo_ref[d_tok:, :], preferred_element_type=F32)
    x1 = x_ref[...] + a
    x1_ref[...] = x1
    xn = _rms(x1, g_ref[...])
    xn_ref[...] = xn.astype(BF16)
    logits = lax.dot_general(xn, wr_ref[...], NT_DIMS, precision=lax.Precision.HIGHEST,
                             preferred_element_type=F32) + br_ref[...]
    tm = logits.shape[0]
    lane = lax.broadcasted_iota(I32, (tm, LANES), 1)
    lane_f = lane.astype(F32)
    is_expert = lane < N_EXPERTS
    is_group = (lane >= N_EXPERTS) & (lane < N_EXPERTS + N_EXPERT_GROUPS)
    lg = jnp.where(is_group, logits, -jnp.inf)
    gmax, glane = _first_max(lg, lane_f)
    pg_sel = 1.0 / jnp.sum(jnp.where(is_group, jnp.exp(lg - gmax), 0.0), axis=1, keepdims=True)
    group = glane - float(N_EXPERTS)
    in_group = is_expert & (jnp.floor(lane_f * (1.0 / EXPERTS_PER_GROUP)) == group)
    le = jnp.where(in_group, logits, -jnp.inf)
    tv1, e1 = _first_max(le, lane_f)
    tv2, e2 = _first_max(jnp.where(lane_f == e1, -jnp.inf, le), lane_f)
    r = jnp.exp(tv2 - tv1)
    w1 = pg_sel / (1.0 + r)
    w2 = pg_sel * r / (1.0 + r)
    route = jnp.where(lane == 0, e1, jnp.where(lane == 1, e2, jnp.where(lane == 2, w1, jnp.where(lane == 3, w2, 0.0))))
    route_ref[...] = route


def _out_proj_router(x, o_tok, o_mem, wo, g, wr_t, br):
    t = x.shape[0]
    d_tok = o_tok.shape[1]
    row = lambda i: (i, 0)
    fix = lambda i: (0, 0)
    return pl.pallas_call(
        _out_proj_router_body,
        grid=(t // TM,),
        in_specs=[pl.BlockSpec((TM, D_MODEL), row), pl.BlockSpec((TM, d_tok), row), pl.BlockSpec((TM, D_MEM), row),
                  pl.BlockSpec((D_MODEL, D_MODEL), fix), pl.BlockSpec((1, D_MODEL), fix),
                  pl.BlockSpec((LANES, D_MODEL), fix), pl.BlockSpec((1, LANES), fix)],
        out_specs=[pl.BlockSpec((TM, D_MODEL), row), pl.BlockSpec((TM, D_MODEL), row), pl.BlockSpec((TM, LANES), row)],
        out_shape=[jax.ShapeDtypeStruct((t, D_MODEL), F32), jax.ShapeDtypeStruct((t, D_MODEL), BF16),
                   jax.ShapeDtypeStruct((t, LANES), F32)],
        compiler_params=_params("parallel"),
        name="out_proj_router",
    )(x, o_tok, o_mem, wo, g, wr_t, br)


def _moe_body(te_ref, nu_ref, xs_ref, wg_ref, wu_ref, wd_ref, ys_ref):
    i = pl.program_id(0)

    @pl.when(i < nu_ref[0])
    def _():
        x = xs_ref[...]
        hg = jnp.dot(x, wg_ref[...], preferred_element_type=F32)
        hu = jnp.dot(x, wu_ref[...], preferred_element_type=F32)
        h = hg * jax.nn.sigmoid(hg) * hu
        ys_ref[...] = jnp.dot(h.astype(BF16), wd_ref[...], preferred_element_type=F32)

    @pl.when(i >= nu_ref[0])
    def _():
        ys_ref[...] = jnp.zeros_like(ys_ref)


def _moe_experts(tile_expert, n_used, xs, wg, wu, wd):
    p = xs.shape[0]
    grid_spec = pltpu.PrefetchScalarGridSpec(
        num_scalar_prefetch=2,
        grid=(p // TM,),
        in_specs=[pl.BlockSpec((TM, D_MODEL), lambda i, te, nu: (i, 0)),
                  pl.BlockSpec((None, D_MODEL, D_EXPERT), lambda i, te, nu: (te[i], 0, 0)),
                  pl.BlockSpec((None, D_MODEL, D_EXPERT), lambda i, te, nu: (te[i], 0, 0)),
                  pl.BlockSpec((None, D_EXPERT, D_MODEL), lambda i, te, nu: (te[i], 0, 0))],
        out_specs=pl.BlockSpec((TM, D_MODEL), lambda i, te, nu: (i, 0)),
    )
    return pl.pallas_call(
        _moe_body,
        grid_spec=grid_spec,
        out_shape=jax.ShapeDtypeStruct((p, D_MODEL), F32),
        compiler_params=_params("arbitrary"),
        name="moe_experts",
    )(tile_expert, n_used, xs, wg, wu, wd)


def _combine_body(x_ref, y1_ref, y2_ref, route_ref, *refs, final_norm):
    route = route_ref[...]
    x2 = x_ref[...] + route[:, 2:3] * y1_ref[...] + route[:, 3:4] * y2_ref[...]
    if final_norm:
        g_ref, o_ref = refs
        o_ref[...] = _rms(x2, g_ref[...])
    else:
        (o_ref,) = refs
        o_ref[...] = x2


def _combine(x1, y1, y2, route, final_g=None):
    t = x1.shape[0]
    row = lambda i: (i, 0)
    final_norm = final_g is not None
    in_specs = [pl.BlockSpec((TM, D_MODEL), row)] * 3 + [pl.BlockSpec((TM, LANES), row)]
    args = [x1, y1, y2, route]
    if final_norm:
        in_specs.append(pl.BlockSpec((1, D_MODEL), lambda i: (0, 0)))
        args.append(final_g)
    return pl.pallas_call(
        functools.partial(_combine_body, final_norm=final_norm),
        grid=(t // TM,),
        in_specs=in_specs,
        out_specs=pl.BlockSpec((TM, D_MODEL), row),
        out_shape=jax.ShapeDtypeStruct((t, D_MODEL), F32),
        compiler_params=_params("parallel"),
        name="moe_combine",
    )(*args)


def _moe_layer(x1, xn_b, route, wg, wu, wd, final_g):
    t = x1.shape[0]
    n_slots = 2 * t + N_EXPERTS * TM
    n_tiles = n_slots // TM
    eid = jnp.concatenate([route[:, 0], route[:, 1]]).astype(I32)
    onehot = (eid[:, None] == jnp.arange(N_EXPERTS, dtype=I32)[None, :]).astype(I32)
    counts = jnp.sum(onehot, axis=0)
    rank = jnp.sum((jnp.cumsum(onehot, axis=0) - onehot) * onehot, axis=1)
    padded = ((counts + TM - 1) // TM) * TM
    ends = jnp.cumsum(padded)
    starts = ends - padded
    dest = starts[eid] + rank
    token = jnp.concatenate([jnp.arange(t, dtype=I32)] * 2)
    src = jnp.zeros((n_slots,), I32).at[dest].set(token)
    tile_start = jnp.arange(n_tiles, dtype=I32) * TM
    tile_expert = jnp.minimum(jnp.searchsorted(ends, tile_start, side="right"), N_EXPERTS - 1).astype(I32)
    n_used = (ends[-1] // TM).astype(I32).reshape(1)
    xs = jnp.take(xn_b, src, axis=0)
    ys = _moe_experts(tile_expert, n_used, xs, wg, wu, wd)
    y1 = jnp.take(ys, dest[:t], axis=0)
    y2 = jnp.take(ys, dest[t:], axis=0)
    return _combine(x1, y1, y2, route, final_g)


def _rope_tables(pos):
    half = ROPE_DIM // 2
    inv_freq = ROPE_THETA ** (-jnp.arange(half, dtype=F32) / half)
    ang = pos.astype(F32)[:, None] * inv_freq[None, :]
    cos, sin = jnp.cos(ang), jnp.sin(ang)
    t = pos.shape[0]
    pad = jnp.zeros((t, HEAD_DIM - ROPE_DIM), F32)
    c = jnp.concatenate([cos, cos, pad + 1.0], axis=1)
    s1 = jnp.concatenate([-sin, jnp.zeros_like(sin), pad], axis=1)
    s2 = jnp.concatenate([jnp.zeros_like(sin), sin, pad], axis=1)
    rep = lambda a: jnp.concatenate([a, a], axis=1)
    return rep(c), rep(s1), rep(s2)


def _router_weights(wg, bg, we, be):
    wr_t = jnp.zeros((LANES, D_MODEL), F32)
    wr_t = wr_t.at[:N_EXPERTS].set(we.T).at[N_EXPERTS:N_EXPERTS + N_EXPERT_GROUPS].set(wg.T)
    br = jnp.zeros((1, LANES), F32)
    br = br.at[0, :N_EXPERTS].set(be).at[0, N_EXPERTS:N_EXPERTS + N_EXPERT_GROUPS].set(bg)
    return wr_t, br


def _head_major_t(a):
    return jnp.transpose(a, (0, 2, 3, 1))


def kernel(x_prompt, x_sample, cache_moba_k, cache_moba_v, state_pool, cache_mem_k, cache_mem_v, page_table,
           mem_prompt, norm1_g, norm2_g, norm_mem_g, final_norm_g, w_in_moba, w_out_moba, w_in_pool, pool_w,
           pool_scale, w_out_pool, w_mem_kv, router_group_w, router_group_b, router_expert_w, router_expert_b,
           w_gate, w_up, w_down):
    s_len = x_prompt.shape[1]
    nb, dec = x_sample.shape[0], x_sample.shape[1]
    n_pages = page_table.shape[1]
    past = n_pages * PAGE_SIZE
    t_sample = nb * dec
    t_all = s_len + t_sample
    sample_block0 = s_len // dec

    x = jnp.concatenate([x_prompt.reshape(s_len, D_MODEL), x_sample.reshape(t_sample, D_MODEL)], axis=0)
    pos = jnp.concatenate([jnp.arange(s_len, dtype=I32), jnp.tile(past + jnp.arange(dec, dtype=I32), nb)])
    rope_c, rope_s1, rope_s2 = _rope_tables(pos)

    mem_kt, mem_vt = [], []
    for i in range(2):
        mk, mv = _in_proj_split(mem_prompt[0], norm_mem_g[i][None], w_mem_kv[i].astype(BF16), D_MEM)
        mem_kt.append(mk.T.reshape(2, LANES, N_MEM))
        mem_vt.append(mv.T.reshape(2, LANES, N_MEM))
    smem_kt = _head_major_t(cache_mem_k.reshape((2 * nb,) + cache_mem_k.shape[2:])).reshape(2, nb, 2, LANES, N_MEM)
    smem_vt = _head_major_t(cache_mem_v.reshape((2 * nb,) + cache_mem_v.shape[2:])).reshape(2, nb, 2, LANES, N_MEM)

    def mem_attend(layer, qm):
        o_p = _mem_attn(qm, mem_kt[layer][None], mem_vt[layer][None], 512, s_len // 512, 0, False)
        o_s = _mem_attn(qm, smem_kt[layer], smem_vt[layer], dec, nb, sample_block0, True)
        return jnp.concatenate([o_p, o_s], axis=0)

    def moe(layer, x1, xn_b, route, final_g):
        return _moe_layer(x1, xn_b, route, w_gate[layer].astype(BF16), w_up[layer].astype(BF16),
                          w_down[layer].astype(BF16), final_g)

    q, k, v, qm, kb, vb, kmean = _in_proj_moba(x, norm1_g[0][None], w_in_moba[0].astype(BF16),
                                               rope_c, rope_s1, rope_s2)
    n_blocks = s_len // MOBA_BLOCK
    kmean_p = jnp.zeros((LANES, D_MOBA), F32).at[:n_blocks].set(kmean[:n_blocks, 0])
    o_prompt = _moba_prompt(q, kb, vb, kmean_p, s_len)

    kc = jnp.transpose(cache_moba_k[0], (0, 2, 3, 1))
    vc = jnp.transpose(cache_moba_v[0], (0, 2, 3, 1))
    sel = _moba_select(page_table, kc, q, dec, sample_block0)[..., :MOBA_TOPK]
    logical = (sel[..., None] * PAGES_PER_BLOCK + jnp.arange(PAGES_PER_BLOCK, dtype=I32)).reshape(nb, -1)
    phys = jnp.take_along_axis(page_table, logical, axis=1).reshape(-1)
    heads4 = lambda a: _head_major_t(a[s_len:].reshape(nb, dec, N_HEADS_MOBA, HEAD_DIM))
    o_t = _moba_sample(phys, kc, vc, heads4(q), heads4(k), heads4(v))
    o_sample = jnp.transpose(o_t, (0, 3, 1, 2)).reshape(t_sample, D_MOBA)
    o_tok = jnp.concatenate([o_prompt, o_sample], axis=0)

    wr_t, br = _router_weights(router_group_w[0], router_group_b[0], router_expert_w[0], router_expert_b[0])
    x1, xn_b, route = _out_proj_router(x, o_tok, mem_attend(0, qm), w_out_moba[0].astype(BF16), norm2_g[0][None],
                                       wr_t, br)
    x = moe(0, x1, xn_b, route, None)

    u, qm = _in_proj_split(x, norm1_g[1][None], w_in_pool[0].astype(BF16), D_POOL)
    w_bd = jnp.zeros((D_POOL, D_POOL), F32)
    for g in range(len(POOL_WINDOWS)):
        lo = g * POOL_GROUP_DIM
        w_bd = w_bd.at[lo:lo + POOL_GROUP_DIM, lo:lo + POOL_GROUP_DIM].set(pool_w[0, g])
    w_bd = w_bd.astype(BF16)
    halo_per_tile = TM // POOL_HALO
    o_prompt = _pool_mix(u, u, w_bd, pool_scale[0][None], TM, s_len // TM, 0,
                         lambda i: (jnp.maximum(i * halo_per_tile - 1, 0), 0), True, False)
    prefix = jnp.pad(state_pool[0], ((0, 0), (POOL_HALO - POOL_STATE, 0), (0, 0)))
    o_sample = _pool_mix(u, prefix, w_bd, pool_scale[0][None], dec, nb, sample_block0,
                         lambda i: (i, 0, 0), False, True)
    o_tok = jnp.concatenate([o_prompt, o_sample], axis=0)

    wr_t, br = _router_weights(router_group_w[1], router_group_b[1], router_expert_w[1], router_expert_b[1])
    x1, xn_b, route = _out_proj_router(x, o_tok, mem_attend(1, qm), w_out_pool[0].astype(BF16), norm2_g[1][None],
                                       wr_t, br)
    y = moe(1, x1, xn_b, route, final_norm_g[None])

    heads5 = lambda a, b, s: a.reshape(1, b, s, N_HEADS_MOBA, HEAD_DIM)
    u_s = u[s_len:].reshape(nb, dec, D_POOL)
    pool_state_sample = jnp.concatenate([state_pool[0], u_s], axis=1)[:, -POOL_STATE:][None]
    mem_heads = lambda ts: jnp.stack([jnp.transpose(a.reshape(N_MEM // HEAD_DIM, HEAD_DIM, N_MEM), (2, 0, 1))
                                      for a in ts])[:, None]
    return (y[:s_len].reshape(1, s_len, D_MODEL), y[s_len:].reshape(nb, dec, D_MODEL),
            heads5(k[:s_len], 1, s_len), heads5(v[:s_len], 1, s_len),
            u[s_len - POOL_STATE:s_len].reshape(1, 1, POOL_STATE, D_POOL),
            mem_heads(mem_kt), mem_heads(mem_vt),
            heads5(k[s_len:], nb, dec), heads5(v[s_len:], nb, dec),
            pool_state_sample)
```
